```python
import math
import jax, jax.numpy as jnp
from jax import lax
import numpy as np

D_MODEL = 1024
BATCH = 4
SEQ = 4096
DEPTH = 1

CHUNK = 64
QBLK = 128
N_HEADS_A = 4
HEAD_DIM_A = 64
V_DIM_A = 2 * HEAD_DIM_A
ATTN_WIDTH = N_HEADS_A * V_DIM_A
CONV_WIDTH = D_MODEL // 2
CONV_K = 31
N_REL_BUCKETS = 32
REL_MAX_DIST = 128
N_GROUPS = 8
EXPERTS_PER_GROUP = 8
N_EXPERTS = N_GROUPS * EXPERTS_PER_GROUP
TOP_K = 2
D_EXPERT = D_MODEL // 2
MOE_BLK = 128
EPS = 1e-6

Q_COLS = N_HEADS_A * 2 * HEAD_DIM_A
K_COLS = N_HEADS_A * 2 * HEAD_DIM_A
V_COLS = ATTN_WIDTH
GLU_COLS = 2 * CONV_WIDTH
GATE_COLS = 2 * D_MODEL
IN_COLS = Q_COLS + K_COLS + V_COLS + GLU_COLS + GATE_COLS

kernel_name = "hybrid_diffattn_conformer_hmoe_block"


def rmsnorm(x, g):
    xf = x.astype(jnp.float32)
    y = xf * lax.rsqrt(jnp.mean(xf * xf, axis=-1, keepdims=True) + EPS)
    return (y * g.astype(jnp.float32)).astype(x.dtype)


def layernorm(x, g, b):
    xf = x.astype(jnp.float32)
    mu = jnp.mean(xf, axis=-1, keepdims=True)
    xc = xf - mu
    y = xc * lax.rsqrt(jnp.mean(xc * xc, axis=-1, keepdims=True) + EPS)
    return (y * g.astype(jnp.float32) + b.astype(jnp.float32)).astype(x.dtype)


def rel_bucket(rel):
    nb = N_REL_BUCKETS // 2
    max_exact = nb // 2
    ret = jnp.where(rel > 0, nb, 0)
    n = jnp.abs(rel)
    nf = jnp.maximum(n, 1).astype(jnp.float32)
    large = max_exact + (jnp.log(nf / max_exact) / math.log(REL_MAX_DIST / max_exact)
                         * (nb - max_exact)).astype(jnp.int32)
    large = jnp.minimum(large, nb - 1)
    return ret + jnp.where(n < max_exact, n, large)


def diff_attention(q, k, v, lam, rel_bias):
    B, S = q.shape[0], q.shape[1]
    nq = S // QBLK
    scale = HEAD_DIM_A ** -0.5
    qb = q.reshape(B, nq, QBLK, N_HEADS_A, 2, HEAD_DIM_A).transpose(1, 0, 2, 3, 4, 5)
    kpos = jnp.arange(S)
    neg = jnp.finfo(jnp.float32).min

    def block(args):
        qblk, i = args
        qpos = i * QBLK + jnp.arange(QBLK)
        s = jnp.einsum('bqhmd,bkhmd->bhmqk', qblk, k).astype(jnp.float32) * scale
        bias = rel_bias.astype(jnp.float32)[rel_bucket(kpos[None, :] - qpos[:, None])]
        s = s + bias.transpose(2, 0, 1)[None, :, None]
        mask = (kpos[None, :] // CHUNK) <= (qpos[:, None] // CHUNK)
        s = jnp.where(mask[None, None, None], s, neg)
        p = jax.nn.softmax(s, axis=-1)
        a = p[:, :, 0] - lam * p[:, :, 1]
        return jnp.einsum('bhqk,bkhv->bqhv', a.astype(v.dtype), v)

    out = lax.map(block, (qb, jnp.arange(nq)))
    return out.transpose(1, 0, 2, 3, 4).reshape(B, S, N_HEADS_A, V_DIM_A)


def causal_depthwise_conv(u, w, b):
    C = u.shape[-1]
    y = lax.conv_general_dilated(u, w[:, None, :].astype(u.dtype), window_strides=(1,),
                                 padding=[(CONV_K - 1, 0)],
                                 dimension_numbers=('NWC', 'WIO', 'NWC'),
                                 feature_group_count=C)
    return y + b.astype(u.dtype)


def hier_moe(h, w_rg, b_rg, w_re, b_re, w_g, w_u, w_d):
    B, S, D = h.shape
    N = B * S
    hf = h.reshape(N, D)
    lg = (hf @ w_rg + b_rg).astype(jnp.float32)
    pg = jax.nn.softmax(lg, axis=-1)
    grp = jnp.argmax(lg, axis=-1)
    pg_top = jnp.take_along_axis(pg, grp[:, None], axis=-1)
    le = (hf @ w_re + b_re).astype(jnp.float32).reshape(N, N_GROUPS, EXPERTS_PER_GROUP)
    le_sel = jnp.take_along_axis(le, grp[:, None, None], axis=1)[:, 0]
    top_v, top_j = lax.top_k(le_sel, TOP_K)
    wts = pg_top * jax.nn.softmax(top_v, axis=-1)
    eid = grp[:, None] * EXPERTS_PER_GROUP + top_j
    A = N * TOP_K
    eid_f = eid.reshape(A)
    tok_f = jnp.repeat(jnp.arange(N), TOP_K)
    w_f = wts.reshape(A)
    order = jnp.argsort(eid_f)
    se, stok, sw = eid_f[order], tok_f[order], w_f[order]
    counts = jax.ops.segment_sum(jnp.ones((A,), jnp.int32), eid_f, num_segments=N_EXPERTS)
    starts = jnp.cumsum(counts) - counts
    pcounts = ((counts + MOE_BLK - 1) // MOE_BLK) * MOE_BLK
    pends = jnp.cumsum(pcounts)
    pstarts = pends - pcounts
    dest = pstarts[se] + (jnp.arange(A) - starts[se])
    P = A + N_EXPERTS * MOE_BLK
    nb = P // MOE_BLK
    row_tok = jnp.full((P,), N, jnp.int32).at[dest].set(stok)
    row_w = jnp.zeros((P,), jnp.float32).at[dest].set(sw)
    blk_e = jnp.minimum(jnp.searchsorted(pends, jnp.arange(nb) * MOE_BLK, side='right'),
                        N_EXPERTS - 1)
    xin = jnp.take(hf, row_tok, axis=0, mode='fill', fill_value=0).reshape(nb, MOE_BLK, D)

    def expert_block(args):
        xb, e = args
        return (jax.nn.silu(xb @ w_g[e]) * (xb @ w_u[e])) @ w_d[e]

    yb = lax.map(expert_block, (xin, blk_e)).reshape(P, D)
    yb = yb * row_w[:, None].astype(yb.dtype)
    out = jnp.zeros((N, D), hf.dtype).at[row_tok].add(yb, mode='drop')
    return out.reshape(B, S, D)


def setup_inputs(seed: int = 0) -> dict:
    key = jax.random.key(seed)
    ks = jax.random.split(key, 26)

    def nrm(k, shape, scale):
        return jax.random.normal(k, shape, jnp.float32) * scale

    L, D = DEPTH, D_MODEL
    return {
        "x": nrm(ks[0], (BATCH, SEQ, D), 1.0),
        "w_in": nrm(ks[1], (L, D, IN_COLS), D ** -0.5),
        "lambda_q1": nrm(ks[2], (L, HEAD_DIM_A), 0.1),
        "lambda_k1": nrm(ks[3], (L, HEAD_DIM_A), 0.1),
        "lambda_q2": nrm(ks[4], (L, HEAD_DIM_A), 0.1),
        "lambda_k2": nrm(ks[5], (L, HEAD_DIM_A), 0.1),
        "g_subln": 1.0 + nrm(ks[6], (L, V_DIM_A), 0.02),
        "rel_bias": nrm(ks[7], (N_REL_BUCKETS, N_HEADS_A), 0.2),
        "w_attn_up": nrm(ks[8], (L, ATTN_WIDTH, D), ATTN_WIDTH ** -0.5),
        "w_dw": nrm(ks[9], (L, CONV_K, CONV_WIDTH), CONV_K ** -0.5),
        "b_dw": nrm(ks[10], (L, CONV_WIDTH), 0.02),
        "g_conv_ln": 1.0 + nrm(ks[11], (L, CONV_WIDTH), 0.02),
        "b_conv_ln": nrm(ks[12], (L, CONV_WIDTH), 0.02),
        "w_conv_pw2": nrm(ks[13], (L, CONV_WIDTH, D), CONV_WIDTH ** -0.5),
        "b_conv_pw2": nrm(ks[14], (L, D), 0.02),
        "w_out": nrm(ks[15], (L, D, D), D ** -0.5),
        "g_mix": 1.0 + nrm(ks[16], (L, D), 0.02),
        "g_ffn": 1.0 + nrm(ks[17], (L, D), 0.02),
        "w_router_group": nrm(ks[18], (L, D, N_GROUPS), D ** -0.5),
        "b_router_group": nrm(ks[19], (L, N_GROUPS), 0.01),
        "w_router_expert": nrm(ks[20], (L, D, N_EXPERTS), D ** -0.5),
        "b_router_expert": nrm(ks[21], (L, N_EXPERTS), 0.01),
        "w_exp_gate": nrm(ks[22], (L, N_EXPERTS, D, D_EXPERT), D ** -0.5),
        "w_exp_up": nrm(ks[23], (L, N_EXPERTS, D, D_EXPERT), D ** -0.5),
        "w_exp_down": nrm(ks[24], (L, N_EXPERTS, D_EXPERT, D), D_EXPERT ** -0.5),
        "g_final": 1.0 + nrm(ks[25], (D,), 0.02),
    }


def reference(x, w_in, lambda_q1, lambda_k1, lambda_q2, lambda_k2, g_subln, rel_bias,
              w_attn_up, w_dw, b_dw, g_conv_ln, b_conv_ln, w_conv_pw2, b_conv_pw2, w_out,
              g_mix, g_ffn, w_router_group, b_router_group, w_router_expert, b_router_expert,
              w_exp_gate, w_exp_up, w_exp_down, g_final):
    B, S, D = x.shape
    o_q = 0
    o_k = o_q + Q_COLS
    o_v = o_k + K_COLS
    o_glu = o_v + V_COLS
    o_gate = o_glu + GLU_COLS
    for l in range(DEPTH):
        lam_init = 0.8 - 0.6 * math.exp(-0.3 * l)
        h = rmsnorm(x, g_mix[l])
        proj = h @ w_in[l]
        q = proj[..., o_q:o_k].reshape(B, S, N_HEADS_A, 2, HEAD_DIM_A)
        k = proj[..., o_k:o_v].reshape(B, S, N_HEADS_A, 2, HEAD_DIM_A)
        v = proj[..., o_v:o_glu].reshape(B, S, N_HEADS_A, V_DIM_A)
        glu = proj[..., o_glu:o_gate]
        gates = jax.nn.sigmoid(proj[..., o_gate:])
        lam = (jnp.exp(jnp.sum(lambda_q1[l].astype(jnp.float32) * lambda_k1[l].astype(jnp.float32)))
               - jnp.exp(jnp.sum(lambda_q2[l].astype(jnp.float32) * lambda_k2[l].astype(jnp.float32)))
               + lam_init)
        o = diff_attention(q, k, v, lam, rel_bias)
        o = rmsnorm(o, g_subln[l]) * (1.0 - lam_init)
        y_attn = o.reshape(B, S, ATTN_WIDTH) @ w_attn_up[l]
        u = glu[..., :CONV_WIDTH] * jax.nn.sigmoid(glu[..., CONV_WIDTH:])
        u = causal_depthwise_conv(u, w_dw[l], b_dw[l])
        u = jax.nn.silu(layernorm(u, g_conv_ln[l], b_conv_ln[l]))
        y_conv = u @ w_conv_pw2[l] + b_conv_pw2[l]
        merged = gates[..., :D] * y_attn + gates[..., D:] * y_conv
        x = x + merged @ w_out[l]
        x = x + hier_moe(rmsnorm(x, g_ffn[l]), w_router_group[l], b_router_group[l],
                         w_router_expert[l], b_router_expert[l],
                         w_exp_gate[l], w_exp_up[l], w_exp_down[l])
    return rmsnorm(x, g_final)
```

```python
import functools
import math

import jax
import jax.numpy as jnp
import numpy as np
from jax import lax
from jax.experimental import pallas as pl
from jax.experimental.pallas import tpu as pltpu

D_MODEL = 1024
CHUNK = 64
N_HEADS = 4
HEAD_DIM = 64
V_DIM = 2 * HEAD_DIM
ATTN_WIDTH = N_HEADS * V_DIM
CONV_WIDTH = D_MODEL // 2
CONV_K = 31
N_REL_BUCKETS = 32
REL_MAX_DIST = 128
N_GROUPS = 8
EXPERTS_PER_GROUP = 8
N_EXPERTS = N_GROUPS * EXPERTS_PER_GROUP
TOP_K = 2
D_EXPERT = D_MODEL // 2
EPS = 1e-6
LAM_INIT = 0.8 - 0.6 * math.exp(-0.3 * 0)

Q_COLS = N_HEADS * 2 * HEAD_DIM
K_COLS = Q_COLS
V_COLS = ATTN_WIDTH
GLU_COLS = 2 * CONV_WIDTH
O_GATE = Q_COLS + K_COLS + V_COLS + GLU_COLS

LANES = 128
ROUTE_W = LANES
MASK_NEG = -1e30
VMEM_LIMIT = 56 * 1024 * 1024

TM_IN = 512
T_ATT = 256
TM_POST = 512
HALO = 32
TM_TOK = 256
BLK = 256

BF16 = jnp.bfloat16
F32 = jnp.float32


def _cparams(sem):
    return pltpu.CompilerParams(dimension_semantics=sem, vmem_limit_bytes=VMEM_LIMIT)


def _rms(x, g):
    return (x * lax.rsqrt(jnp.mean(x * x, axis=-1, keepdims=True) + EPS)) * g


def _inproj_kernel(x_ref, g_ref, w_ref, q_ref, k_ref, v_ref, u_ref):
    h = _rms(x_ref[...], g_ref[...]).astype(BF16)
    proj = jnp.dot(h, w_ref[...], preferred_element_type=F32)
    q_ref[...] = (proj[:, :Q_COLS] * (HEAD_DIM ** -0.5)).astype(BF16)
    k_ref[...] = proj[:, Q_COLS:Q_COLS + K_COLS].astype(BF16)
    v_ref[...] = proj[:, Q_COLS + K_COLS:Q_COLS + K_COLS + V_COLS].astype(BF16)
    o_glu = Q_COLS + K_COLS + V_COLS
    a = proj[:, o_glu:o_glu + CONV_WIDTH]
    b = proj[:, o_glu + CONV_WIDTH:o_glu + GLU_COLS]
    u_ref[...] = a * jax.nn.sigmoid(b)


def _inproj(x2, g_mix, w_qkvu):
    n = x2.shape[0]
    wcols = w_qkvu.shape[1]
    row = lambda i: (i, 0)
    full = lambda i: (0, 0)
    return pl.pallas_call(
        _inproj_kernel,
        grid=(n // TM_IN,),
        in_specs=[pl.BlockSpec((TM_IN, D_MODEL), row),
                  pl.BlockSpec((1, D_MODEL), full),
                  pl.BlockSpec((D_MODEL, wcols), full)],
        out_specs=[pl.BlockSpec((TM_IN, Q_COLS), row),
                   pl.BlockSpec((TM_IN, K_COLS), row),
                   pl.BlockSpec((TM_IN, V_COLS), row),
                   pl.BlockSpec((TM_IN, CONV_WIDTH), row)],
        out_shape=[jax.ShapeDtypeStruct((n, Q_COLS), BF16),
                   jax.ShapeDtypeStruct((n, K_COLS), BF16),
                   jax.ShapeDtypeStruct((n, V_COLS), BF16),
                   jax.ShapeDtypeStruct((n, CONV_WIDTH), F32)],
        compiler_params=_cparams(("parallel",)),
        name="inproj",
    )(x2, g_mix, w_qkvu)


def _rel_bucket(rel):
    nb = N_REL_BUCKETS // 2
    max_exact = nb // 2
    ret = jnp.where(rel > 0, nb, 0)
    n = jnp.abs(rel)
    nf = jnp.maximum(n, 1).astype(jnp.float32)
    large = max_exact + (jnp.log(nf / max_exact) / math.log(REL_MAX_DIST / max_exact)
                         * (nb - max_exact)).astype(jnp.int32)
    large = jnp.minimum(large, nb - 1)
    return ret + jnp.where(n < max_exact, n, large)


def _bias_tiles(rel_bias, t):
    assert t >= REL_MAX_DIST and t % CHUNK == 0
    rb = rel_bias.astype(F32)
    far = rb[_rel_bucket(jnp.asarray(-REL_MAX_DIST, jnp.int32))]
    qpos = jnp.arange(t)[:, None]
    tiles = []
    for off in (0, 1):
        kpos = jnp.arange(t)[None, :] - off * t
        b = rb[_rel_bucket(kpos - qpos)] - far
        if off == 0:
            mask = (kpos // CHUNK) <= (qpos // CHUNK)
            b = jnp.where(mask[:, :, None], b, MASK_NEG)
        tiles.append(b.transpose(2, 0, 1))
    return jnp.stack(tiles, axis=1)


def _attn_kernel(q_ref, k_ref, v_ref, bias_ref, lam_ref, gs_ref, o_ref, m_scr, l_scr, acc_scr):
    t = T_ATT
    i = pl.program_id(2)
    q = q_ref[...]
    lane = lax.broadcasted_iota(jnp.int32, q.shape, 1)
    zero = jnp.zeros_like(q)
    q2 = jnp.concatenate([jnp.where(lane < HEAD_DIM, q, zero),
                          jnp.where(lane >= HEAD_DIM, q, zero)], axis=0)
    m_scr[...] = jnp.full(m_scr.shape, -jnp.inf, F32)
    l_scr[...] = jnp.zeros(l_scr.shape, F32)
    acc_scr[...] = jnp.zeros(acc_scr.shape, F32)

    def step(j, bias):
        start = pl.multiple_of(j * t, t)
        kb = k_ref[pl.ds(start, t), :]
        vb = v_ref[pl.ds(start, t), :]
        s = lax.dot_general(q2, kb, (((1,), (1,)), ((), ())), preferred_element_type=F32)
        if bias is not None:
            s = s + jnp.concatenate([bias, bias], axis=0)
        m_prev = m_scr[...]
        m_next = jnp.maximum(m_prev, jnp.max(s, axis=1, keepdims=True))
        p = jnp.exp(s - m_next)
        alpha = jnp.exp(m_prev - m_next)
        l_scr[...] = alpha * l_scr[...] + jnp.sum(p, axis=1, keepdims=True)
        acc_scr[...] = alpha * acc_scr[...] + jnp.dot(p.astype(BF16), vb, preferred_element_type=F32)
        m_scr[...] = m_next

    def far_body(j, carry):
        step(j, None)
        return carry

    lax.fori_loop(0, jnp.maximum(i - 1, 0), far_body, 0)

    @pl.when(i >= 1)
    def _():
        step(i - 1, bias_ref[1])

    step(i, bias_ref[0])

    lp = lam_ref[...]
    lam = (jnp.exp(jnp.sum(lp[0:1] * lp[1:2], axis=-1, keepdims=True))
           - jnp.exp(jnp.sum(lp[2:3] * lp[3:4], axis=-1, keepdims=True)) + LAM_INIT)
    o_maps = acc_scr[...] / l_scr[...]
    o = o_maps[:t] - lam * o_maps[t:]
    o = _rms(o, gs_ref[...]) * (1.0 - LAM_INIT)
    o_ref[...] = o.astype(BF16)


def _attention(q, k, v, bias_tiles, lam_params, g_subln):
    bsz, s, _ = q.shape
    t = T_ATT
    nq = s // t
    return pl.pallas_call(
        _attn_kernel,
        grid=(bsz, N_HEADS, nq),
        in_specs=[pl.BlockSpec((None, t, V_DIM), lambda b, h, i: (b, i, h)),
                  pl.BlockSpec((None, s, V_DIM), lambda b, h, i: (b, 0, h)),
                  pl.BlockSpec((None, s, V_DIM), lambda b, h, i: (b, 0, h)),
                  pl.BlockSpec((None, 2, t, t), lambda b, h, i: (h, 0, 0, 0)),
                  pl.BlockSpec((4, HEAD_DIM), lambda b, h, i: (0, 0)),
                  pl.BlockSpec((1, V_DIM), lambda b, h, i: (0, 0))],
        out_specs=pl.BlockSpec((None, t, V_DIM), lambda b, h, i: (b, i, h)),
        out_shape=jax.ShapeDtypeStruct((bsz, s, ATTN_WIDTH), BF16),
        scratch_shapes=[pltpu.VMEM((2 * t, 1), F32),
                        pltpu.VMEM((2 * t, 1), F32),
                        pltpu.VMEM((2 * t, V_DIM), F32)],
        compiler_params=_cparams(("parallel", "parallel", "arbitrary")),
        name="attn",
    )(q, k, v, bias_tiles, lam_params, g_subln)


def _post_kernel(x_ref, o_ref, u_ref, uh_ref, gmix_ref, wgate_ref, wup_ref, wdw_ref, bdw_ref,
                 gln_ref, bln_ref, wpw_ref, bpw_ref, wout_ref, gffn_ref, wr_ref, br_ref,
                 xmid_ref, h2_ref, route_ref, ext_scr):
    tm = TM_POST
    i = pl.program_id(1)
    x = x_ref[...]
    hb = _rms(x, gmix_ref[...]).astype(BF16)
    gates = jax.nn.sigmoid(jnp.dot(hb, wgate_ref[...], preferred_element_type=F32))
    y_attn = jnp.dot(o_ref[...], wup_ref[...], preferred_element_type=F32)

    halo = uh_ref[...]
    ext_scr[0:HALO, :] = jnp.where(i > 0, halo, jnp.zeros_like(halo))
    ext_scr[HALO:HALO + tm, :] = u_ref[...]
    wdw = wdw_ref[...]
    acc = jnp.zeros((tm, CONV_WIDTH), F32)
    for j in range(CONV_K):
        off = HALO - (CONV_K - 1) + j
        acc = acc + ext_scr[off:off + tm, :] * wdw[j:j + 1, :]
    c = acc + bdw_ref[...]
    mu = jnp.mean(c, axis=-1, keepdims=True)
    cc = c - mu
    cn = cc * lax.rsqrt(jnp.mean(cc * cc, axis=-1, keepdims=True) + EPS)
    cn = cn * gln_ref[...] + bln_ref[...]
    act = cn * jax.nn.sigmoid(cn)
    y_conv = jnp.dot(act.astype(BF16), wpw_ref[...], preferred_element_type=F32) + bpw_ref[...]

    merged = gates[:, :D_MODEL] * y_attn + gates[:, D_MODEL:] * y_conv
    xm = x + jnp.dot(merged.astype(BF16), wout_ref[...], preferred_element_type=F32)
    xmid_ref[...] = xm
    h2 = _rms(xm, gffn_ref[...])
    h2_ref[...] = h2

    lg = jnp.dot(h2, wr_ref[...], preferred_element_type=F32,
                 precision=lax.Precision.HIGHEST) + br_ref[...]
    lane = lax.broadcasted_iota(jnp.int32, lg.shape, 1)
    lanef = lane.astype(F32)
    big = jnp.float32(1e9)
    neg = jnp.float32(-jnp.inf)
    is_g = (lane >= N_EXPERTS) & (lane < N_EXPERTS + N_GROUPS)
    glog = jnp.where(is_g, lg, neg)
    gmax = jnp.max(glog, axis=1, keepdims=True)
    gidx = jnp.min(jnp.where(glog == gmax, lanef, big), axis=1, keepdims=True) - N_EXPERTS
    pg_top = 1.0 / jnp.sum(jnp.where(is_g, jnp.exp(lg - gmax), 0.0), axis=1, keepdims=True)
    in_grp = (lane < N_EXPERTS) & (lax.shift_right_logical(lane, 3).astype(F32) == gidx)
    e1 = jnp.where(in_grp, lg, neg)
    v1 = jnp.max(e1, axis=1, keepdims=True)
    j1 = jnp.min(jnp.where(e1 == v1, lanef, big), axis=1, keepdims=True)
    e2 = jnp.where(lanef == j1, neg, e1)
    v2 = jnp.max(e2, axis=1, keepdims=True)
    j2 = jnp.min(jnp.where(e2 == v2, lanef, big), axis=1, keepdims=True)
    tt = jnp.exp(v2 - v1)
    w1 = pg_top * (1.0 / (1.0 + tt))
    w2 = pg_top * (tt / (1.0 + tt))
    route_ref[...] = jnp.where(lane == 0, j1, jnp.where(lane == 1, j2, jnp.where(
        lane == 2, w1, jnp.where(lane == 3, w2, 0.0))))


def _post(x, o, u, g_mix, w_gate, w_up, w_dw, b_dw, g_ln, b_ln, w_pw, b_pw, w_out, g_ffn, w_r, b_r):
    bsz, s, _ = x.shape
    tm = TM_POST
    nt = s // tm
    hpt = tm // HALO
    row = lambda b, i: (b, i, 0)
    full = lambda b, i: (0, 0)
    wspec = lambda a: pl.BlockSpec(a.shape, full)
    return pl.pallas_call(
        _post_kernel,
        grid=(bsz, nt),
        in_specs=[pl.BlockSpec((None, tm, D_MODEL), row),
                  pl.BlockSpec((None, tm, ATTN_WIDTH), row),
                  pl.BlockSpec((None, tm, CONV_WIDTH), row),
                  pl.BlockSpec((None, HALO, CONV_WIDTH), lambda b, i: (b, jnp.maximum(i * hpt - 1, 0), 0)),
                  wspec(g_mix), wspec(w_gate), wspec(w_up), wspec(w_dw), wspec(b_dw), wspec(g_ln),
                  wspec(b_ln), wspec(w_pw), wspec(b_pw), wspec(w_out), wspec(g_ffn), wspec(w_r), wspec(b_r)],
        out_specs=[pl.BlockSpec((None, tm, D_MODEL), row),
                   pl.BlockSpec((None, tm, D_MODEL), row),
                   pl.BlockSpec((None, tm, ROUTE_W), row)],
        out_shape=[jax.ShapeDtypeStruct((bsz, s, D_MODEL), F32),
                   jax.ShapeDtypeStruct((bsz, s, D_MODEL), F32),
                   jax.ShapeDtypeStruct((bsz, s, ROUTE_W), F32)],
        scratch_shapes=[pltpu.VMEM((HALO + tm, CONV_WIDTH), F32)],
        compiler_params=_cparams(("parallel", "arbitrary")),
        name="post",
    )(x, o, u, u, g_mix, w_gate, w_up, w_dw, b_dw, g_ln, b_ln, w_pw, b_pw, w_out, g_ffn, w_r, b_r)


def _row_copy_loop(n, make_copy, unroll=8):
    def start(r, c):
        make_copy(r).start()
        return c

    def wait(r, c):
        make_copy(r).wait()
        return c

    kw = {} if unroll is None else {"unroll": unroll}
    lax.fori_loop(0, n, start, 0, **kw)
    lax.fori_loop(0, n, wait, 0, **kw)


def _dispatch_kernel(dest_ref, zblk_ref, nz_ref, h2_ref, xin_ref, zero_scr, sem, zsem):
    @pl.when(pl.program_id(0) == 0)
    def _():
        zero_scr[...] = jnp.zeros(zero_scr.shape, F32)

        def zcopy(e):
            row0 = pl.multiple_of(zblk_ref[0, e] * BLK, BLK)
            return pltpu.make_async_copy(zero_scr, xin_ref.at[pl.ds(row0, BLK)], zsem)

        _row_copy_loop(nz_ref[0, 0], zcopy, unroll=None)

    def copy(a):
        return pltpu.make_async_copy(h2_ref.at[pl.ds(lax.shift_right_logical(a, 1), 1)],
                                     xin_ref.at[pl.ds(dest_ref[0, a], 1)], sem)

    _row_copy_loop(TOP_K * TM_TOK, copy)


def _dispatch(dest3, zblk, nz, h2, p_rows):
    n = h2.shape[0]
    nt = n // TM_TOK
    return pl.pallas_call(
        _dispatch_kernel,
        grid=(nt,),
        in_specs=[pl.BlockSpec((None, 1, TOP_K * TM_TOK), lambda i: (i, 0, 0), memory_space=pltpu.SMEM),
                  pl.BlockSpec(zblk.shape, lambda i: (0, 0), memory_space=pltpu.SMEM),
                  pl.BlockSpec((1, 1), lambda i: (0, 0), memory_space=pltpu.SMEM),
                  pl.BlockSpec((TM_TOK, D_MODEL), lambda i: (i, 0))],
        out_specs=pl.BlockSpec(memory_space=pl.ANY),
        out_shape=jax.ShapeDtypeStruct((p_rows, D_MODEL), F32),
        scratch_shapes=[pltpu.VMEM((BLK, D_MODEL), F32),
                        pltpu.SemaphoreType.DMA,
                        pltpu.SemaphoreType.DMA],
        compiler_params=_cparams(("arbitrary",)),
        name="dispatch",
    )(dest3, zblk, nz, h2)


def _experts_kernel(blk_e_ref, nused_ref, x_ref, wg_ref, wu_ref, wd_ref, y_ref):
    @pl.when(pl.program_id(0) < nused_ref[0])
    def _():
        xb = x_ref[...].astype(BF16)
        g = jnp.dot(xb, wg_ref[...].astype(BF16), preferred_element_type=F32)
        u = jnp.dot(xb, wu_ref[...].astype(BF16), preferred_element_type=F32)
        hmid = (g * jax.nn.sigmoid(g)) * u
        y_ref[...] = jnp.dot(hmid.astype(BF16), wd_ref[...].astype(BF16), preferred_element_type=F32)

    @pl.when(pl.program_id(0) >= nused_ref[0])
    def _():
        y_ref[...] = jnp.zeros(y_ref.shape, F32)


def _experts(blk_e, nused, xin, w_g, w_u, w_d):
    p_rows = xin.shape[0]
    nblk = p_rows // BLK
    xmap = lambda b, be, nu: (jnp.minimum(b, nu[0] - 1), 0)
    wmap = lambda b, be, nu: (be[b], 0, 0)
    grid_spec = pltpu.PrefetchScalarGridSpec(
        num_scalar_prefetch=2,
        grid=(nblk,),
        in_specs=[pl.BlockSpec((BLK, D_MODEL), xmap),
                  pl.BlockSpec((None, D_MODEL, D_EXPERT), wmap),
                  pl.BlockSpec((None, D_MODEL, D_EXPERT), wmap),
                  pl.BlockSpec((None, D_EXPERT, D_MODEL), wmap)],
        out_specs=pl.BlockSpec((BLK, D_MODEL), lambda b, be, nu: (b, 0)),
    )
    return pl.pallas_call(
        _experts_kernel,
        grid_spec=grid_spec,
        out_shape=jax.ShapeDtypeStruct((p_rows, D_MODEL), F32),
        compiler_params=_cparams(("arbitrary",)),
        name="experts",
    )(blk_e, nused, xin, w_g, w_u, w_d)


def _combine_kernel(dest_ref, xmid_ref, route_ref, gfin_ref, y_ref, out_ref, ybuf, sem):
    def copy(a):
        return pltpu.make_async_copy(y_ref.at[pl.ds(dest_ref[0, a], 1)],
                                     ybuf.at[a & 1, pl.ds(lax.shift_right_logical(a, 1), 1)], sem)

    _row_copy_loop(TOP_K * TM_TOK, copy)
    r = route_ref[...]
    moe = r[:, 2:3] * ybuf[0] + r[:, 3:4] * ybuf[1]
    out_ref[...] = _rms(xmid_ref[...] + moe, gfin_ref[...])


def _combine(dest3, xmid, route, g_final, y):
    n = xmid.shape[0]
    nt = n // TM_TOK
    return pl.pallas_call(
        _combine_kernel,
        grid=(nt,),
        in_specs=[pl.BlockSpec((None, 1, TOP_K * TM_TOK), lambda i: (i, 0, 0), memory_space=pltpu.SMEM),
                  pl.BlockSpec((TM_TOK, D_MODEL), lambda i: (i, 0)),
                  pl.BlockSpec((TM_TOK, ROUTE_W), lambda i: (i, 0)),
                  pl.BlockSpec((1, D_MODEL), lambda i: (0, 0)),
                  pl.BlockSpec(memory_space=pl.ANY)],
        out_specs=pl.BlockSpec((TM_TOK, D_MODEL), lambda i: (i, 0)),
        out_shape=jax.ShapeDtypeStruct((n, D_MODEL), F32),
        scratch_shapes=[pltpu.VMEM((TOP_K, TM_TOK, D_MODEL), F32),
                        pltpu.SemaphoreType.DMA],
        compiler_params=_cparams(("arbitrary",)),
        name="combine",
    )(dest3, xmid, route, g_final, y)


def _routing_plan(route, n):
    a_total = n * TOP_K
    eid = route[:, :TOP_K].astype(jnp.int32).reshape(a_total)
    onehot = (eid[:, None] == jnp.arange(N_EXPERTS, dtype=jnp.int32)[None, :]).astype(jnp.int32)
    csum = jnp.cumsum(onehot, axis=0)
    rank = jnp.sum((csum - onehot) * onehot, axis=1)
    counts = csum[-1]
    nblk_e = (counts + BLK - 1) // BLK
    bend = jnp.cumsum(nblk_e)
    bstart = bend - nblk_e
    dest = bstart[eid] * BLK + rank
    nblk = a_total // BLK + N_EXPERTS
    nused = bend[-1]
    blk = jnp.minimum(jnp.arange(nblk, dtype=jnp.int32), nused - 1)
    blk_e = jnp.minimum(jnp.searchsorted(bend, blk, side="right"), N_EXPERTS - 1).astype(jnp.int32)
    last_blk = jnp.clip(bend - 1, 0, nblk - 1)
    tail = jnp.minimum(nused + jnp.arange(N_EXPERTS, dtype=jnp.int32), nblk - 1)
    zblk = jnp.concatenate([last_blk, tail]).astype(jnp.int32).reshape(1, 2 * N_EXPERTS)
    nz = (N_EXPERTS + nblk - nused).astype(jnp.int32).reshape(1, 1)
    return dest.astype(jnp.int32), blk_e, nused.reshape(1).astype(jnp.int32), zblk, nz, nblk


def kernel(x, w_in, lambda_q1, lambda_k1, lambda_q2, lambda_k2, g_subln, rel_bias, w_attn_up, w_dw, b_dw, g_conv_ln, b_conv_ln, w_conv_pw2, b_conv_pw2, w_out, g_mix, g_ffn, w_router_group, b_router_group, w_router_expert, b_router_expert, w_exp_gate, w_exp_up, w_exp_down, g_final):
    bsz, s, d = x.shape
    n = bsz * s
    assert d == D_MODEL and w_in.shape[0] == 1
    assert s % T_ATT == 0 and s % TM_POST == 0 and n % TM_IN == 0 and n % TM_TOK == 0
    l = 0
    w_in_b = w_in[l].astype(BF16)
    x2 = x.reshape(n, d)
    row2 = lambda v: v.reshape(1, -1).astype(F32)

    q, k, v, u = _inproj(x2, row2(g_mix[l]), w_in_b[:, :O_GATE])
    lam_params = jnp.stack([lambda_q1[l], lambda_k1[l], lambda_q2[l], lambda_k2[l]]).astype(F32)
    o = _attention(q.reshape(bsz, s, Q_COLS), k.reshape(bsz, s, K_COLS), v.reshape(bsz, s, V_COLS),
                   _bias_tiles(rel_bias, T_ATT), lam_params, row2(g_subln[l]))

    w_r = jnp.zeros((d, ROUTE_W), F32)
    w_r = w_r.at[:, :N_EXPERTS].set(w_router_expert[l]).at[:, N_EXPERTS:N_EXPERTS + N_GROUPS].set(w_router_group[l])
    b_r = jnp.zeros((1, ROUTE_W), F32)
    b_r = b_r.at[0, :N_EXPERTS].set(b_router_expert[l]).at[0, N_EXPERTS:N_EXPERTS + N_GROUPS].set(b_router_group[l])
    xmid, h2, route = _post(
        x, o, u.reshape(bsz, s, CONV_WIDTH), row2(g_mix[l]), w_in_b[:, O_GATE:],
        w_attn_up[l].astype(BF16), w_dw[l].astype(F32), row2(b_dw[l]), row2(g_conv_ln[l]),
        row2(b_conv_ln[l]), w_conv_pw2[l].astype(BF16), row2(b_conv_pw2[l]), w_out[l].astype(BF16),
        row2(g_ffn[l]), w_r, b_r)

    route2 = route.reshape(n, ROUTE_W)
    dest, blk_e, nused, zblk, nz, nblk = _routing_plan(route2, n)
    dest3 = dest.reshape(n // TM_TOK, 1, TOP_K * TM_TOK)
    xin = _dispatch(dest3, zblk, nz, h2.reshape(n, d), nblk * BLK)
    y = _experts(blk_e, nused, xin, w_exp_gate[l], w_exp_up[l], w_exp_down[l])
    out = _combine(dest3, xmid.reshape(n, d), route2, row2(g_final), y)
    return out.reshape(bsz, s, d)
```
